```python
import jax, jax.numpy as jnp
from jax import lax
import numpy as np

D_MODEL = 1024
BATCH = 32
SEQ = 2048
DEPTH = 4

HEAD_SIZE = 64
RWKV_WIDTH = D_MODEL // 2
RWKV_HEADS = RWKV_WIDTH // HEAD_SIZE
CONV_WIDTH = D_MODEL - RWKV_WIDTH
CONV_GROUPS = CONV_WIDTH // HEAD_SIZE
MIX_WIDTH = RWKV_WIDTH + CONV_WIDTH
CONV_TAPS = 31


def _lora_dim(factor, power):
    return max(32, int(round(factor * (D_MODEL ** power) / 32)) * 32)


DECAY_LORA = _lora_dim(1.8, 0.5)
ICLR_LORA = _lora_dim(1.8, 0.5)
VALUE_LORA = _lora_dim(1.3, 0.5)
GATE_LORA = _lora_dim(0.6, 0.8)
SHIFT_COLS_FIRST = 3 * RWKV_WIDTH + DECAY_LORA + ICLR_LORA + GATE_LORA
SHIFT_COLS_REST = SHIFT_COLS_FIRST + VALUE_LORA
IN_COLS_FIRST = SHIFT_COLS_FIRST + 2 * CONV_WIDTH
IN_COLS_REST = SHIFT_COLS_REST + 2 * CONV_WIDTH
FFN_HIDDEN = ((8 * D_MODEL // 3 + 255) // 256) * 256
RMS_EPS = 1e-6
LNX_EPS = 64e-5
LN_EPS = 1e-5

kernel_name = "hybrid_rwkv7_conformer_conv_parallel"


def _rmsnorm(x, g):
    xf = x.astype(jnp.float32)
    y = xf * lax.rsqrt(jnp.mean(xf * xf, axis=-1, keepdims=True) + RMS_EPS)
    return (y * g.astype(jnp.float32)).astype(x.dtype)


def _token_shift(p, mu):
    prev = jnp.pad(p, ((0, 0), (1, 0), (0, 0)))[:, :-1]
    return p + (prev - p) * mu


def _rwkv7_scan(r, decay, k, v, a_vec, b_vec):
    def step(state, inp):
        r_t, w_t, k_t, v_t, a_t, b_t = inp
        sa = jnp.einsum('bhij,bhj->bhi', state, a_t)
        state = (state * w_t[:, :, None, :] + sa[..., None] * b_t[:, :, None, :]
                 + v_t[..., None] * k_t[:, :, None, :])
        y_t = jnp.einsum('bhij,bhj->bhi', state, r_t)
        return state, y_t
    B, _, H, N = r.shape
    s0 = jnp.zeros((B, H, N, N), jnp.float32)
    seq_major = tuple(jnp.moveaxis(t, 1, 0) for t in (r, decay, k, v, a_vec, b_vec))
    _, ys = lax.scan(step, s0, seq_major)
    return jnp.moveaxis(ys, 0, 1)


def _rwkv7_group(p, v_first, w0, w2, a0, a2, g2, v0, v2, k_k, k_a, r_k, lnx_g, lnx_b):
    B, S, _ = p.shape
    p = p.astype(jnp.float32)
    RW = RWKV_WIDTH
    r = p[..., :RW]
    k = p[..., RW:2 * RW]
    v = p[..., 2 * RW:3 * RW]
    o = 3 * RW
    hw = p[..., o:o + DECAY_LORA]
    o += DECAY_LORA
    ha = p[..., o:o + ICLR_LORA]
    o += ICLR_LORA
    hg = p[..., o:o + GATE_LORA]
    o += GATE_LORA
    w = -jax.nn.softplus(-(w0 + jnp.tanh(hw) @ w2)) - 0.5
    decay = jnp.exp(-jnp.exp(w))
    a = jax.nn.sigmoid(a0 + ha @ a2)
    g = jax.nn.sigmoid(hg) @ g2
    if v0 is None:
        v_first = v
    else:
        hv = p[..., o:]
        v = v + (v_first - v) * jax.nn.sigmoid(v0 + hv @ v2)
    heads = lambda t: t.reshape(B, S, RWKV_HEADS, HEAD_SIZE)
    kk = heads(k * k_k)
    kk = kk / jnp.maximum(jnp.linalg.norm(kk, axis=-1, keepdims=True), 1e-12)
    k = k * (1.0 + (a - 1.0) * k_a)
    rh, kh, vh, ah = heads(r), heads(k), heads(v), heads(a)
    y = _rwkv7_scan(rh, heads(decay), kh, vh, -kk, kk * ah)
    mean = jnp.mean(y, axis=-1, keepdims=True)
    var = jnp.mean(jnp.square(y - mean), axis=-1, keepdims=True)
    yn = ((y - mean) * lax.rsqrt(var + LNX_EPS)).reshape(B, S, RW) * lnx_g + lnx_b
    bonus = jnp.sum(rh * kh * r_k.reshape(RWKV_HEADS, HEAD_SIZE), axis=-1, keepdims=True) * vh
    out = (yn + bonus.reshape(B, S, RW)) * g
    return out, v_first


def _conv_group(pc, conv_w, conv_b, ln_g, ln_b):
    u = pc[..., :CONV_WIDTH] * jax.nn.sigmoid(pc[..., CONV_WIDTH:])
    c = lax.conv_general_dilated(
        u, conv_w.reshape(CONV_TAPS, 1, CONV_WIDTH).astype(u.dtype),
        window_strides=(1,), padding=[(CONV_TAPS - 1, 0)],
        dimension_numbers=('NWC', 'WIO', 'NWC'), feature_group_count=CONV_WIDTH)
    c = (c + conv_b).astype(jnp.float32)
    mean = jnp.mean(c, axis=-1, keepdims=True)
    var = jnp.mean(jnp.square(c - mean), axis=-1, keepdims=True)
    c = (c - mean) * lax.rsqrt(var + LN_EPS) * ln_g + ln_b
    return jax.nn.silu(c)


def setup_inputs(seed: int = 0) -> dict:
    key = jax.random.key(seed)
    ks = jax.random.split(key, 32)
    f32 = jnp.float32
    nrm = lambda k, shape, s: jax.random.normal(k, shape, f32) * s
    L, RW, CW = DEPTH, RWKV_WIDTH, CONV_WIDTH
    return {
        "x": nrm(ks[0], (BATCH, SEQ, D_MODEL), 1.0),
        "norm_mix_g": 1.0 + nrm(ks[1], (L, D_MODEL), 0.02),
        "w_in_first": nrm(ks[2], (D_MODEL, IN_COLS_FIRST), D_MODEL ** -0.5),
        "w_in_rest": nrm(ks[3], (L - 1, D_MODEL, IN_COLS_REST), D_MODEL ** -0.5),
        "shift_mu_first": jax.random.uniform(ks[4], (SHIFT_COLS_FIRST,), f32),
        "shift_mu_rest": jax.random.uniform(ks[5], (L - 1, SHIFT_COLS_REST), f32),
        "decay_base": jax.random.uniform(ks[6], (L, RW), f32, -6.0, -1.0),
        "decay_up": nrm(ks[7], (L, DECAY_LORA, RW), 0.1 * DECAY_LORA ** -0.5),
        "iclr_base": nrm(ks[8], (L, RW), 0.5),
        "iclr_up": nrm(ks[9], (L, ICLR_LORA, RW), 0.1 * ICLR_LORA ** -0.5),
        "gate_up": nrm(ks[10], (L, GATE_LORA, RW), GATE_LORA ** -0.5),
        "vres_base": 1.0 + nrm(ks[11], (L - 1, RW), 0.1),
        "vres_up": nrm(ks[12], (L - 1, VALUE_LORA, RW), 0.1 * VALUE_LORA ** -0.5),
        "k_k": 0.85 + nrm(ks[13], (L, RW), 0.05),
        "k_a": 1.0 + nrm(ks[14], (L, RW), 0.05),
        "r_k": nrm(ks[15], (L, RW), 0.1),
        "lnx_g": 1.0 + nrm(ks[16], (L, RW), 0.02),
        "lnx_b": nrm(ks[17], (L, RW), 0.02),
        "conv_w": nrm(ks[18], (L, CONV_TAPS, CW), CONV_TAPS ** -0.5),
        "conv_b": nrm(ks[19], (L, CW), 0.02),
        "conv_ln_g": 1.0 + nrm(ks[20], (L, CW), 0.02),
        "conv_ln_b": nrm(ks[21], (L, CW), 0.02),
        "w_out": nrm(ks[22], (L, MIX_WIDTH, D_MODEL), MIX_WIDTH ** -0.5),
        "norm_ffn_g": 1.0 + nrm(ks[23], (L, D_MODEL), 0.02),
        "w_ffn_in": nrm(ks[24], (L, D_MODEL, 2 * FFN_HIDDEN), D_MODEL ** -0.5),
        "w_ffn_out": nrm(ks[25], (L, FFN_HIDDEN, D_MODEL), FFN_HIDDEN ** -0.5),
        "norm_final_g": 1.0 + nrm(ks[26], (D_MODEL,), 0.02),
    }


def reference(x, norm_mix_g, w_in_first, w_in_rest, shift_mu_first, shift_mu_rest,
              decay_base, decay_up, iclr_base, iclr_up, gate_up, vres_base, vres_up,
              k_k, k_a, r_k, lnx_g, lnx_b, conv_w, conv_b, conv_ln_g, conv_ln_b,
              w_out, norm_ffn_g, w_ffn_in, w_ffn_out, norm_final_g):
    h = x
    v_first = None
    for l in range(DEPTH):
        xn = _rmsnorm(h, norm_mix_g[l])
        if l == 0:
            w_in, mu, v0, v2 = w_in_first, shift_mu_first, None, None
        else:
            w_in, mu = w_in_rest[l - 1], shift_mu_rest[l - 1]
            v0, v2 = vres_base[l - 1], vres_up[l - 1]
        proj = xn @ w_in
        n_shift = proj.shape[-1] - 2 * CONV_WIDTH
        p_rwkv = _token_shift(proj[..., :n_shift], mu)
        p_conv = proj[..., n_shift:]
        y_rwkv, v_first = _rwkv7_group(
            p_rwkv, v_first, decay_base[l], decay_up[l], iclr_base[l], iclr_up[l],
            gate_up[l], v0, v2, k_k[l], k_a[l], r_k[l], lnx_g[l], lnx_b[l])
        y_conv = _conv_group(p_conv, conv_w[l], conv_b[l], conv_ln_g[l], conv_ln_b[l])
        mixed = jnp.concatenate([y_rwkv.astype(h.dtype), y_conv.astype(h.dtype)], axis=-1)
        h = h + mixed @ w_out[l]
        xn = _rmsnorm(h, norm_ffn_g[l])
        gu = xn @ w_ffn_in[l]
        h = h + (jax.nn.silu(gu[..., :FFN_HIDDEN]) * gu[..., FFN_HIDDEN:]) @ w_ffn_out[l]
    return _rmsnorm(h, norm_final_g)
```

```python
import functools

import jax
import jax.numpy as jnp
from jax import lax
from jax.experimental import pallas as pl
from jax.experimental.pallas import tpu as pltpu

F32 = jnp.float32
BF16 = jnp.bfloat16

D_MODEL = 1024
HEAD = 64
RW = 512
NH = RW // HEAD
CW = 512
TAPS = 31
DECAY_LORA, ICLR_LORA, GATE_LORA, VALUE_LORA = 64, 64, 160, 32
LORA_W = DECAY_LORA + ICLR_LORA + GATE_LORA + VALUE_LORA
LORA_PAD = 384
RWKV_COLS = 3 * RW + LORA_PAD
FFN_H = 2816
RMS_EPS = 1e-6
LNX_EPS = 64e-5
LN_EPS = 1e-5
LANES = 128
HALO = 32
VMEM_LIMIT = 56 * 1024 * 1024


def _cparams(sem):
    return pltpu.CompilerParams(dimension_semantics=sem, vmem_limit_bytes=VMEM_LIMIT)


def _full(shape):
    n = len(shape)
    return pl.BlockSpec(shape, lambda *_: (0,) * n)


def _dot(a, b):
    return jnp.dot(a, b, preferred_element_type=F32)


def _segsum(x, ones_bd):
    hi = x.astype(BF16)
    lo = (x - hi.astype(F32)).astype(BF16)
    return _dot(hi, ones_bd) + _dot(lo, ones_bd)


def _inproj_kernel(x_ref, g_ref, w_ref, pr_ref, pc_ref):
    x = x_ref[...]
    ms = jnp.mean(x * x, axis=-1, keepdims=True)
    xn = (x * lax.rsqrt(ms + RMS_EPS) * g_ref[...]).astype(BF16)
    pr_ref[...] = _dot(xn, w_ref[:, :RWKV_COLS])
    pc_ref[...] = _dot(xn, w_ref[:, RWKV_COLS:])


def _inproj(h, g, w, tm=512):
    m = h.shape[0]
    ncols = w.shape[1]
    return pl.pallas_call(
        _inproj_kernel,
        grid=(m // tm,),
        in_specs=[pl.BlockSpec((tm, D_MODEL), lambda i: (i, 0)),
                  _full((1, D_MODEL)),
                  _full((D_MODEL, ncols))],
        out_specs=[pl.BlockSpec((tm, RWKV_COLS), lambda i: (i, 0)),
                   pl.BlockSpec((tm, 2 * CW), lambda i: (i, 0))],
        out_shape=[jax.ShapeDtypeStruct((m, RWKV_COLS), F32),
                   jax.ShapeDtypeStruct((m, 2 * CW), F32)],
        compiler_params=_cparams(("parallel",)),
        name="inproj",
    )(h, g, w)


def _prep_kernel(first, p_ref, pprev_ref, mu_ref, wl_ref, vec_ref, ones_ref, *rest):
    if first:
        vf_ref = None
        outs = rest
    else:
        vf_ref, outs = rest[0], rest[1:]
    r_ref, w_ref, k_ref, v_ref, a_ref, b_ref, g_ref, bonus_ref = outs
    ts = p_ref.shape[0]

    p = p_ref[...]
    prev_last = jnp.where(pl.program_id(1) == 0, 0.0, pprev_ref[7:8, :])
    row = lax.broadcasted_iota(jnp.int32, p.shape, 0)
    prev = jnp.where(row == 0, prev_last, pltpu.roll(p, 1, axis=0))
    ps = p + (prev - p) * mu_ref[...]

    r = ps[:, :RW]
    k = ps[:, RW:2 * RW]
    v = ps[:, 2 * RW:3 * RW]
    lo = ps[:, 3 * RW:]
    col = lax.broadcasted_iota(jnp.int32, lo.shape, 1)
    gate_lo = DECAY_LORA + ICLR_LORA
    act = jnp.where(col < DECAY_LORA, jnp.tanh(lo),
                    jnp.where((col >= gate_lo) & (col < gate_lo + GATE_LORA),
                              jax.nn.sigmoid(lo), lo))
    up = _dot(act.astype(BF16), wl_ref[...])

    w0, a0, v0 = vec_ref[0:1, :], vec_ref[1:2, :], vec_ref[2:3, :]
    k_k, k_a, r_k = vec_ref[3:4, :], vec_ref[4:5, :], vec_ref[5:6, :]

    wlog = -jax.nn.softplus(-(w0 + up[:, :RW])) - 0.5
    decay = jnp.exp(-jnp.exp(wlog))
    a = jax.nn.sigmoid(a0 + up[:, RW:2 * RW])
    g = up[:, 2 * RW:3 * RW]
    if not first:
        v = v + (vf_ref[...] - v) * jax.nn.sigmoid(v0 + up[:, 3 * RW:])
    ones_bd = ones_ref[...]
    kk = k * k_k
    nrm = jnp.sqrt(_segsum(kk * kk, ones_bd))
    kk = kk / jnp.maximum(nrm, 1e-12)
    k2 = k * (1.0 + (a - 1.0) * k_a)

    r_ref[...] = r
    w_ref[...] = decay
    k_ref[...] = k2
    v_ref[...] = v
    a_ref[...] = -kk
    b_ref[...] = kk * a
    g_ref[...] = g
    bonus_ref[...] = _segsum(r * k2 * r_k, ones_bd) * v


def _prep(first, pr, mu, w_lora, vecs, ones_bd, v_first, batch, seq, ts=256):
    m = pr.shape[0]
    ts = min(ts, seq)
    nblk = seq // ts
    tok = lambda b, i: (b * nblk + i, 0)
    prev_map = lambda b, i: (jnp.maximum(b * (seq // 8) + i * (ts // 8) - 1, 0), 0)
    in_specs = [pl.BlockSpec((ts, RWKV_COLS), tok),
                pl.BlockSpec((8, RWKV_COLS), prev_map),
                _full((1, RWKV_COLS)),
                _full((LORA_PAD, 4 * RW)),
                _full((8, RW)),
                _full((RW, RW))]
    args = [pr, pr, mu, w_lora, vecs, ones_bd]
    if not first:
        in_specs.append(pl.BlockSpec((ts, RW), tok))
        args.append(v_first)
    return pl.pallas_call(
        functools.partial(_prep_kernel, first),
        grid=(batch, nblk),
        in_specs=in_specs,
        out_specs=[pl.BlockSpec((ts, RW), tok)] * 8,
        out_shape=[jax.ShapeDtypeStruct((m, RW), F32)] * 8,
        compiler_params=_cparams(("parallel", "parallel")),
        name="rwkv_prep",
    )(*args)


def _scan_kernel(r_ref, w_ref, k_ref, v_ref, a_ref, b_ref, y_ref, st_ref):
    tt = r_ref.shape[0]

    @pl.when(pl.program_id(1) == 0)
    def _():
        st_ref[...] = jnp.zeros_like(st_ref)

    def step(t, carry):
        u = jnp.zeros((HEAD, LANES), F32)
        for j in range(HEAD):
            u = u + st_ref[j] * a_ref[t, pl.ds(j, 1), :]
        v = v_ref[t]
        y = jnp.zeros((HEAD, LANES), F32)
        for j in range(HEAD):
            s = (st_ref[j] * w_ref[t, pl.ds(j, 1), :] + u * b_ref[t, pl.ds(j, 1), :]
                 + v * k_ref[t, pl.ds(j, 1), :])
            st_ref[j] = s
            y = y + s * r_ref[t, pl.ds(j, 1), :]
        y_ref[t] = y
        return carry

    lax.fori_loop(0, tt, step, 0)


def _scan(r, w, k, v, a, b, tt=32):
    seq, _, chains = r.shape
    tt = min(tt, seq)
    spec = pl.BlockSpec((tt, HEAD, LANES), lambda c, i: (i, 0, c))
    return pl.pallas_call(
        _scan_kernel,
        grid=(chains // LANES, seq // tt),
        in_specs=[spec] * 6,
        out_specs=spec,
        out_shape=jax.ShapeDtypeStruct((seq, HEAD, chains), F32),
        scratch_shapes=[pltpu.VMEM((HEAD, HEAD, LANES), F32)],
        compiler_params=_cparams(("parallel", "arbitrary")),
        name="rwkv_scan",
    )(r, w, k, v, a, b)


def _mixout_kernel(y_ref, bonus_ref, g_ref, pc_ref, h_ref, vec_ref, cw_ref, ones_ref, wo_ref,
                   o_ref, ubuf_ref):
    ts = y_ref.shape[0]
    lnx_g, lnx_b = vec_ref[0:1, :], vec_ref[1:2, :]
    conv_b, cln_g, cln_b = vec_ref[2:3, :], vec_ref[3:4, :], vec_ref[4:5, :]

    ones_bd = ones_ref[...]
    y = y_ref[...]
    mean = _segsum(y, ones_bd) * (1.0 / HEAD)
    yc = y - mean
    var = _segsum(yc * yc, ones_bd) * (1.0 / HEAD)
    yn = yc * lax.rsqrt(var + LNX_EPS) * lnx_g + lnx_b
    out_r = (yn + bonus_ref[...]) * g_ref[...]

    @pl.when(pl.program_id(1) == 0)
    def _():
        ubuf_ref[0:HALO, :] = jnp.zeros((HALO, CW), F32)

    pc = pc_ref[...]
    ubuf_ref[HALO:, :] = pc[:, :CW] * jax.nn.sigmoid(pc[:, CW:])
    base = HALO - (TAPS - 1)
    c = jnp.zeros((ts, CW), F32)
    for tap in range(TAPS):
        c = c + ubuf_ref[base + tap:base + tap + ts, :] * cw_ref[tap:tap + 1, :]
    ubuf_ref[0:HALO, :] = ubuf_ref[ts:ts + HALO, :]
    c = c + conv_b
    cm = jnp.mean(c, axis=-1, keepdims=True)
    cc = c - cm
    cv = jnp.mean(cc * cc, axis=-1, keepdims=True)
    cn = cc * lax.rsqrt(cv + LN_EPS) * cln_g + cln_b
    out_c = cn * jax.nn.sigmoid(cn)

    o_ref[...] = (h_ref[...] + _dot(out_r.astype(BF16), wo_ref[:RW, :])
                  + _dot(out_c.astype(BF16), wo_ref[RW:, :]))


def _mixout(y, bonus, g, pc, h, vecs, conv_w, ones_bd, w_out, batch, seq, ts=256):
    m = y.shape[0]
    ts = min(ts, seq)
    nblk = seq // ts
    tok = lambda b, i: (b * nblk + i, 0)
    return pl.pallas_call(
        _mixout_kernel,
        grid=(batch, nblk),
        in_specs=[pl.BlockSpec((ts, RW), tok),
                  pl.BlockSpec((ts, RW), tok),
                  pl.BlockSpec((ts, RW), tok),
                  pl.BlockSpec((ts, 2 * CW), tok),
                  pl.BlockSpec((ts, D_MODEL), tok),
                  _full((8, RW)),
                  _full((HALO, CW)),
                  _full((RW, RW)),
                  _full((D_MODEL, D_MODEL))],
        out_specs=pl.BlockSpec((ts, D_MODEL), tok),
        out_shape=jax.ShapeDtypeStruct((m, D_MODEL), F32),
        scratch_shapes=[pltpu.VMEM((ts + HALO, CW), F32)],
        compiler_params=_cparams(("parallel", "arbitrary")),
        name="mix_out",
    )(y, bonus, g, pc, h, vecs, conv_w, ones_bd, w_out)


def _ffn_kernel(final, chunk, x_ref, g_ref, wi_ref, wo_ref, gf_ref, o_ref):
    x = x_ref[...]
    ms = jnp.mean(x * x, axis=-1, keepdims=True)
    xn = (x * lax.rsqrt(ms + RMS_EPS) * g_ref[...]).astype(BF16)
    acc = x
    for c in range(FFN_H // chunk):
        gate = _dot(xn, wi_ref[:, c * chunk:(c + 1) * chunk])
        upv = _dot(xn, wi_ref[:, FFN_H + c * chunk:FFN_H + (c + 1) * chunk])
        act = (gate * jax.nn.sigmoid(gate) * upv).astype(BF16)
        acc = acc + _dot(act, wo_ref[c * chunk:(c + 1) * chunk, :])
    if final:
        ms2 = jnp.mean(acc * acc, axis=-1, keepdims=True)
        acc = acc * lax.rsqrt(ms2 + RMS_EPS) * gf_ref[...]
    o_ref[...] = acc


def _ffn(h, g, w_in, w_out, g_final, final, tm=512, chunk=256):
    m = h.shape[0]
    return pl.pallas_call(
        functools.partial(_ffn_kernel, final, chunk),
        grid=(m // tm,),
        in_specs=[pl.BlockSpec((tm, D_MODEL), lambda i: (i, 0)),
                  _full((1, D_MODEL)),
                  _full((D_MODEL, 2 * FFN_H)),
                  _full((FFN_H, D_MODEL)),
                  _full((1, D_MODEL))],
        out_specs=pl.BlockSpec((tm, D_MODEL), lambda i: (i, 0)),
        out_shape=jax.ShapeDtypeStruct((m, D_MODEL), F32),
        compiler_params=_cparams(("parallel",)),
        name="ffn",
    )(h, g, w_in, w_out, g_final)


def _to_chains(t, batch, seq):
    return t.reshape(batch, seq, NH, HEAD).transpose(1, 3, 0, 2).reshape(seq, HEAD, batch * NH)


def _from_chains(t, batch, seq):
    return t.reshape(seq, HEAD, batch, NH).transpose(2, 0, 3, 1).reshape(batch * seq, RW)


def _pack_lora(decay_up, iclr_up, gate_up, vres_up):
    w = jnp.zeros((LORA_PAD, 4 * RW), F32)
    o = 0
    w = w.at[o:o + DECAY_LORA, 0:RW].set(decay_up)
    o += DECAY_LORA
    w = w.at[o:o + ICLR_LORA, RW:2 * RW].set(iclr_up)
    o += ICLR_LORA
    w = w.at[o:o + GATE_LORA, 2 * RW:3 * RW].set(gate_up)
    o += GATE_LORA
    if vres_up is not None:
        w = w.at[o:o + VALUE_LORA, 3 * RW:].set(vres_up)
    return w.astype(BF16)


def _rows(*vs):
    rows = [v.reshape(1, -1).astype(F32) for v in vs]
    rows += [jnp.zeros_like(rows[0])] * (8 - len(rows))
    return jnp.concatenate(rows, axis=0)


def kernel(x, norm_mix_g, w_in_first, w_in_rest, shift_mu_first, shift_mu_rest, decay_base, decay_up, iclr_base, iclr_up, gate_up, vres_base, vres_up, k_k, k_a, r_k, lnx_g, lnx_b, conv_w, conv_b, conv_ln_g, conv_ln_b, w_out, norm_ffn_g, w_ffn_in, w_ffn_out, norm_final_g):
    batch, seq, _ = x.shape
    depth = norm_mix_g.shape[0]
    m = batch * seq
    h = x.reshape(m, D_MODEL)
    seg = jnp.arange(RW) // HEAD
    ones_bd = (seg[:, None] == seg[None, :]).astype(BF16)
    v_first = None
    for l in range(depth):
        first = l == 0
        w_in = w_in_first if first else w_in_rest[l - 1]
        mu = shift_mu_first if first else shift_mu_rest[l - 1]
        n_shift = w_in.shape[1] - 2 * CW
        w_pad = jnp.concatenate(
            [w_in[:, :n_shift], jnp.zeros((D_MODEL, RWKV_COLS - n_shift), F32), w_in[:, n_shift:]],
            axis=1).astype(BF16)
        mu_pad = jnp.concatenate([mu, jnp.zeros((RWKV_COLS - n_shift,), F32)]).reshape(1, RWKV_COLS)
        w_lora = _pack_lora(decay_up[l], iclr_up[l], gate_up[l], None if first else vres_up[l - 1])
        v0 = jnp.zeros((RW,), F32) if first else vres_base[l - 1]
        prep_vecs = _rows(decay_base[l], iclr_base[l], v0, k_k[l], k_a[l], r_k[l])
        mix_vecs = _rows(lnx_g[l], lnx_b[l], conv_b[l], conv_ln_g[l], conv_ln_b[l])
        cw_pad = jnp.concatenate([conv_w[l], jnp.zeros((HALO - TAPS, CW), F32)], axis=0)

        pr, pc = _inproj(h, norm_mix_g[l].reshape(1, D_MODEL), w_pad)
        r, w, k, v, a, b, g, bonus = _prep(first, pr, mu_pad, w_lora, prep_vecs, ones_bd,
                                           v_first, batch, seq)
        if first:
            v_first = v
        y = _scan(*(_to_chains(t, batch, seq) for t in (r, w, k, v, a, b)))
        y = _from_chains(y, batch, seq)
        h = _mixout(y, bonus, g, pc, h, mix_vecs, cw_pad, ones_bd, w_out[l].astype(BF16), batch, seq)
        h = _ffn(h, norm_ffn_g[l].reshape(1, D_MODEL), w_ffn_in[l].astype(BF16),
                 w_ffn_out[l].astype(BF16), norm_final_g.reshape(1, D_MODEL), l == depth - 1)
    return h.reshape(batch, seq, D_MODEL)
```
